```python
import math
import jax, jax.numpy as jnp
from jax import lax
import numpy as np

D_MODEL = 1024
BATCH = 4
SEQ = 4096
DEPTH = 4
DEC_BATCH = 32
DEC_SEQ = 32
PAST_LEN = 2048

CHUNK = 64
CONV_W = 4
EPS = 1e-6
MIXERS = ('delta', 'fox', 'ssd')

DN_HEADS = 8
DN_HD = 128
DN_W = DN_HEADS * DN_HD
FOX_HEADS = 16
FOX_HD = 64
FOX_W = FOX_HEADS * FOX_HD
Q_BLOCK = 128
SSD_DINNER = 2 * D_MODEL
SSD_HD = 64
SSD_HEADS = SSD_DINNER // SSD_HD
SSD_GROUPS = 4
SSD_HPG = SSD_HEADS // SSD_GROUPS
SSD_N = 128
SSD_CONV_DIM = SSD_DINNER + 2 * SSD_GROUPS * SSD_N
N_MEM = 256
MEM_HEADS = 4
MEM_HD = 256
MEM_W = MEM_HEADS * MEM_HD
PEER_HEADS = 8
PEER_NKEYS = 128
PEER_N = PEER_NKEYS * PEER_NKEYS
PEER_DQ = 256
PEER_TOPK = 16
PEER_TOK_BLOCK = 256
W_IN_DN = 4 * DN_W + 2 * DN_HEADS + MEM_W
W_IN_FOX = 3 * FOX_W + FOX_HEADS + MEM_W
W_IN_SSD = SSD_DINNER + SSD_CONV_DIM + SSD_HEADS + MEM_W

kernel_name = 'hybrid_stream_encoder_step'


def rmsnorm(x, g):
    xf = x.astype(jnp.float32)
    y = xf * lax.rsqrt(jnp.mean(xf * xf, axis=-1, keepdims=True) + EPS)
    return (y * g.astype(jnp.float32)).astype(x.dtype)


def l2norm(x):
    xf = x.astype(jnp.float32)
    return xf * lax.rsqrt(jnp.sum(xf * xf, axis=-1, keepdims=True) + EPS)


def causal_conv(u, buf, w, b=None):
    L = u.shape[1]
    full = jnp.concatenate([buf.astype(u.dtype), u], axis=1)
    out = full[:, 0:L] * w[0]
    for j in range(1, CONV_W):
        out = out + full[:, j:j + L] * w[j]
    if b is not None:
        out = out + b
    return out, full[:, -(CONV_W - 1):]


def split_chunks(a, axis, c):
    shp = a.shape
    a = a.reshape(shp[:axis] + (shp[axis] // c, c) + shp[axis + 1:])
    return jnp.moveaxis(a, axis, 0)


def merge_chunks(a, axis):
    a = jnp.moveaxis(a, 0, axis)
    shp = a.shape
    return a.reshape(shp[:axis] + (shp[axis] * shp[axis + 1],) + shp[axis + 2:])


def gated_delta_scan(q, k, v, g, beta, s0):
    L = q.shape[2]
    c = min(CHUNK, L)
    tri = jnp.tril(jnp.ones((c, c), dtype=bool))
    strict = jnp.tril(jnp.ones((c, c), dtype=bool), -1)
    eye = jnp.eye(c, dtype=jnp.float32)

    def step(s, xs):
        qc, kc, vc, gc, bc = xs
        gam = jnp.cumsum(gc, axis=-1)
        dec = jnp.exp(jnp.where(tri, gam[..., :, None] - gam[..., None, :], -jnp.inf))
        eg = jnp.exp(gam)[..., None]
        kk = jnp.einsum('bhtd,bhid->bhti', kc, kc)
        a_mat = jnp.where(strict, bc[..., None] * dec * kk, 0.0) + eye
        rhs = bc[..., None] * (vc - eg * jnp.einsum('bhtd,bhde->bhte', kc, s))
        delta = lax.linalg.triangular_solve(a_mat, rhs, left_side=True, lower=True,
                                            unit_diagonal=True)
        qk = dec * jnp.einsum('bhtd,bhid->bhti', qc, kc)
        o = eg * jnp.einsum('bhtd,bhde->bhte', qc, s) + jnp.einsum('bhti,bhie->bhte', qk, delta)
        g_last = gam[..., -1:]
        s = jnp.exp(g_last)[..., None] * s + jnp.einsum(
            'bhid,bhie->bhde', kc * jnp.exp(g_last - gam)[..., None], delta)
        return s, o

    xs = (split_chunks(q, 2, c), split_chunks(k, 2, c), split_chunks(v, 2, c),
          split_chunks(g, 2, c), split_chunks(beta, 2, c))
    s, o = lax.scan(step, s0, xs)
    return merge_chunks(o, 2), s


def delta_mixer(pc, conv_w, a_log, dt_bias, norm_g, s0, conv_buf):
    B, L, _ = pc.shape
    f32 = jnp.float32
    qkv, z, a, b = jnp.split(pc, [3 * DN_W, 4 * DN_W, 4 * DN_W + DN_HEADS], axis=-1)
    qkv, new_buf = causal_conv(qkv, conv_buf, conv_w)
    q, k, v = jnp.split(jax.nn.silu(qkv), 3, axis=-1)
    q = l2norm(jnp.swapaxes(q.reshape(B, L, DN_HEADS, DN_HD), 1, 2)) * DN_HD ** -0.5
    k = l2norm(jnp.swapaxes(k.reshape(B, L, DN_HEADS, DN_HD), 1, 2))
    v = jnp.swapaxes(v.reshape(B, L, DN_HEADS, DN_HD), 1, 2).astype(f32)
    g = -jnp.exp(a_log.astype(f32)) * jax.nn.softplus(a.astype(f32) + dt_bias.astype(f32))
    beta = jax.nn.sigmoid(b.astype(f32))
    o, s_new = gated_delta_scan(q, k, v, jnp.swapaxes(g, 1, 2), jnp.swapaxes(beta, 1, 2),
                                s0.astype(f32))
    o = jnp.swapaxes(o, 1, 2)
    o = rmsnorm(o, norm_g) * jax.nn.silu(z.reshape(B, L, DN_HEADS, DN_HD).astype(f32))
    return o.reshape(B, L, DN_W).astype(pc.dtype), (s_new.astype(pc.dtype), new_buf)


def fox_block(qb, cqb, qpos, k, v, ck, kpos):
    s = jnp.einsum('bqhd,bkhd->bhqk', qb, k).astype(jnp.float32) * FOX_HD ** -0.5
    s = s + jnp.swapaxes(cqb, 1, 2)[..., :, None] - jnp.swapaxes(ck, 1, 2)[..., None, :]
    s = jnp.where(kpos[None, :] <= qpos[:, None], s, -jnp.inf)
    p = jax.nn.softmax(s, axis=-1).astype(v.dtype)
    return jnp.einsum('bhqk,bkhd->bqhd', p, v)


def fox_mixer(pc, bf, past):
    B, L, _ = pc.shape
    q, k, v, f = jnp.split(pc, [FOX_W, 2 * FOX_W, 3 * FOX_W], axis=-1)
    q = q.reshape(B, L, FOX_HEADS, FOX_HD)
    k = k.reshape(B, L, FOX_HEADS, FOX_HD)
    v = v.reshape(B, L, FOX_HEADS, FOX_HD)
    logf = jax.nn.log_sigmoid(f.astype(jnp.float32) + bf.astype(jnp.float32))
    if past is None:
        cum = jnp.cumsum(logf, axis=1)
        pos = jnp.arange(L)

        def block(n):
            s0 = n * Q_BLOCK
            qb = lax.dynamic_slice_in_dim(q, s0, Q_BLOCK, axis=1)
            cqb = lax.dynamic_slice_in_dim(cum, s0, Q_BLOCK, axis=1)
            return fox_block(qb, cqb, s0 + jnp.arange(Q_BLOCK), k, v, cum, pos)

        o = lax.map(block, jnp.arange(L // Q_BLOCK))
        o = jnp.moveaxis(o, 0, 1).reshape(B, L, FOX_W)
    else:
        ck, cv, clf = past
        P = ck.shape[1]
        kall = jnp.concatenate([ck.astype(k.dtype), k], axis=1)
        vall = jnp.concatenate([cv.astype(v.dtype), v], axis=1)
        cum = jnp.cumsum(jnp.concatenate([clf.astype(jnp.float32), logf], axis=1), axis=1)
        o = fox_block(q, cum[:, P:], P + jnp.arange(L), kall, vall, cum,
                      jnp.arange(P + L)).reshape(B, L, FOX_W)
    return o, (k, v, logf.astype(pc.dtype))


def ssd_scan(x, dt, da, bm, cm, h0):
    L = x.shape[3]
    c = min(CHUNK, L)
    tri = jnp.tril(jnp.ones((c, c), dtype=bool))

    def step(h, xs):
        xc, dtc, dac, bc, cc = xs
        gam = jnp.cumsum(dac, axis=-1)
        dec = jnp.exp(jnp.where(tri, gam[..., :, None] - gam[..., None, :], -jnp.inf))
        cb = jnp.einsum('bgtn,bgin->bgti', cc, bc)
        xdt = xc * dtc[..., None]
        y = jnp.einsum('bgjti,bgjip->bgjtp', dec * cb[:, :, None], xdt) + jnp.exp(gam)[..., None] * jnp.einsum('bgtn,bgjpn->bgjtp', cc, h)
        g_last = gam[..., -1:]
        h = jnp.exp(g_last)[..., None] * h + jnp.einsum(
            'bgjip,bgin->bgjpn', xdt * jnp.exp(g_last - gam)[..., None], bc)
        return h, y

    xs = (split_chunks(x, 3, c), split_chunks(dt, 3, c), split_chunks(da, 3, c),
          split_chunks(bm, 2, c), split_chunks(cm, 2, c))
    h, y = lax.scan(step, h0, xs)
    return merge_chunks(y, 3), h


def ssd_mixer(pc, conv_w, conv_b, a_log, dt_bias, d_skip, norm_g, h0, conv_buf):
    B, L, _ = pc.shape
    f32 = jnp.float32
    z, xbc, dt = jnp.split(pc, [SSD_DINNER, SSD_DINNER + SSD_CONV_DIM], axis=-1)
    xbc, new_buf = causal_conv(xbc, conv_buf, conv_w, conv_b)
    xbc = jax.nn.silu(xbc.astype(f32))
    xs, bm, cm = jnp.split(xbc, [SSD_DINNER, SSD_DINNER + SSD_GROUPS * SSD_N], axis=-1)
    x5 = xs.reshape(B, L, SSD_GROUPS, SSD_HPG, SSD_HD).transpose(0, 2, 3, 1, 4)
    bm = bm.reshape(B, L, SSD_GROUPS, SSD_N).transpose(0, 2, 1, 3)
    cm = cm.reshape(B, L, SSD_GROUPS, SSD_N).transpose(0, 2, 1, 3)
    dt = jax.nn.softplus(dt.astype(f32) + dt_bias.astype(f32))
    dt = dt.reshape(B, L, SSD_GROUPS, SSD_HPG).transpose(0, 2, 3, 1)
    a = -jnp.exp(a_log.astype(f32)).reshape(SSD_GROUPS, SSD_HPG, 1)
    h0 = h0.astype(f32).reshape(B, SSD_GROUPS, SSD_HPG, SSD_HD, SSD_N)
    y, h = ssd_scan(x5, dt, dt * a, bm, cm, h0)
    y = y + d_skip.astype(f32).reshape(SSD_GROUPS, SSD_HPG, 1, 1) * x5
    y = y.transpose(0, 3, 1, 2, 4).reshape(B, L, SSD_DINNER)
    y = rmsnorm(y * jax.nn.silu(z.astype(f32)), norm_g)
    return y.astype(pc.dtype), (h.reshape(B, SSD_HEADS, SSD_HD, SSD_N).astype(pc.dtype), new_buf)


def mem_attend(q, mk, mv):
    s = jnp.einsum('bqhd,bkhd->bhqk', q, mk.astype(q.dtype)).astype(jnp.float32) * MEM_HD ** -0.5
    p = jax.nn.softmax(s, axis=-1).astype(q.dtype)
    return jnp.einsum('bhqk,bkhd->bqhd', p, mv.astype(q.dtype))


def peer_ffn(x, wq, k1, k2, u, vt):
    B, L, D = x.shape
    T = B * L
    nblk = -(-T // PEER_TOK_BLOCK)
    xt = jnp.pad(x.reshape(T, D), ((0, nblk * PEER_TOK_BLOCK - T), (0, 0)))
    xt = xt.reshape(nblk, PEER_TOK_BLOCK, D)
    half = PEER_DQ // 2

    def block(xb):
        q = (xb @ wq).reshape(PEER_TOK_BLOCK, PEER_HEADS, PEER_DQ).astype(jnp.float32)
        s1 = jnp.einsum('thd,nd->thn', q[..., :half], k1.astype(jnp.float32))
        s2 = jnp.einsum('thd,nd->thn', q[..., half:], k2.astype(jnp.float32))
        v1, i1 = lax.top_k(s1, PEER_TOPK)
        v2, i2 = lax.top_k(s2, PEER_TOPK)
        cand_s = (v1[..., :, None] + v2[..., None, :]).reshape(PEER_TOK_BLOCK, PEER_HEADS, PEER_TOPK * PEER_TOPK)
        cand_i = (i1[..., :, None] * PEER_NKEYS + i2[..., None, :]).reshape(PEER_TOK_BLOCK, PEER_HEADS, PEER_TOPK * PEER_TOPK)
        top_s, pos = lax.top_k(cand_s, PEER_TOPK)
        eidx = jnp.take_along_axis(cand_i, pos, axis=-1)
        gate = jax.nn.softmax(top_s, axis=-1)
        act = jax.nn.gelu(jnp.einsum('td,thkd->thk', xb, u[eidx]).astype(jnp.float32), approximate=False)
        return jnp.einsum('thk,thkd->td', (gate * act).astype(xb.dtype), vt[eidx])

    y = lax.map(block, xt)
    return y.reshape(nblk * PEER_TOK_BLOCK, D)[:T].reshape(B, L, D)


def trunk_layer(kind, x, mem_k, mem_v, state, mp, g_mix, g_ffn, pq, pk1, pk2, pu, pv):
    B, L, _ = x.shape
    hn = rmsnorm(x, g_mix)
    proj = hn @ mp['w_in']
    pc, mq = proj[..., :-MEM_W], proj[..., -MEM_W:]
    if kind == 'delta':
        o_mix, new_state = delta_mixer(pc, mp['conv'], mp['alog'], mp['dtb'], mp['ng'], state[0], state[1])
    elif kind == 'fox':
        o_mix, new_state = fox_mixer(pc, mp['bf'], state)
    else:
        o_mix, new_state = ssd_mixer(pc, mp['conv'], mp['convb'], mp['alog'], mp['dtb'], mp['d'], mp['ng'], state[0], state[1])
    o_mem = mem_attend(mq.reshape(B, L, MEM_HEADS, MEM_HD), mem_k, mem_v).reshape(B, L, MEM_W)
    x = x + jnp.concatenate([o_mix, o_mem], axis=-1) @ mp['w_out']
    x = x + peer_ffn(rmsnorm(x, g_ffn), pq, pk1, pk2, pu, pv)
    return x, new_state


def setup_inputs(seed: int = 0) -> dict:
    key = jax.random.key(seed)
    ks = iter(jax.random.split(key, 64))
    f32 = jnp.float32

    def nrm(shape, scale=1.0):
        return scale * jax.random.normal(next(ks), shape, f32)

    def gain(shape):
        return 1.0 + 0.02 * jax.random.normal(next(ks), shape, f32)

    def a_log(n):
        return jnp.log(jax.random.uniform(next(ks), (n,), f32, 1.0, 16.0))

    def dt_bias(n):
        dt = jnp.exp(jax.random.uniform(next(ks), (n,), f32, math.log(1e-3), math.log(1e-1)))
        return dt + jnp.log(-jnp.expm1(-dt))

    inv = D_MODEL ** -0.5
    return {
        'x_prompt': nrm((BATCH, SEQ, D_MODEL)),
        'x_sample': nrm((DEC_BATCH, DEC_SEQ, D_MODEL)),
        'cache_mem_k': nrm((DEPTH, DEC_BATCH, N_MEM, MEM_HEADS, MEM_HD)),
        'cache_mem_v': nrm((DEPTH, DEC_BATCH, N_MEM, MEM_HEADS, MEM_HD)),
        'state_l0_delta': nrm((DEC_BATCH, DN_HEADS, DN_HD, DN_HD), DN_HD ** -0.5),
        'state_l0_conv': nrm((DEC_BATCH, CONV_W - 1, 3 * DN_W)),
        'cache_l1_k': nrm((DEC_BATCH, PAST_LEN, FOX_HEADS, FOX_HD)),
        'cache_l1_v': nrm((DEC_BATCH, PAST_LEN, FOX_HEADS, FOX_HD)),
        'cache_l1_logf': jax.nn.log_sigmoid(3.0 + nrm((DEC_BATCH, PAST_LEN, FOX_HEADS))),
        'state_l2_ssm': nrm((DEC_BATCH, SSD_HEADS, SSD_HD, SSD_N), 0.1),
        'state_l2_conv': nrm((DEC_BATCH, CONV_W - 1, SSD_CONV_DIM)),
        'state_l3_delta': nrm((DEC_BATCH, DN_HEADS, DN_HD, DN_HD), DN_HD ** -0.5),
        'state_l3_conv': nrm((DEC_BATCH, CONV_W - 1, 3 * DN_W)),
        'mem_prompt': nrm((BATCH, N_MEM, D_MODEL)),
        'g_mix': gain((DEPTH, D_MODEL)),
        'g_mem': gain((DEPTH, D_MODEL)),
        'w_mem_k': nrm((DEPTH, D_MODEL, MEM_W), inv),
        'w_mem_v': nrm((DEPTH, D_MODEL, MEM_W), inv),
        'g_ffn': gain((DEPTH, D_MODEL)),
        'peer_wq': nrm((DEPTH, D_MODEL, PEER_HEADS * PEER_DQ), inv),
        'peer_k1': nrm((DEPTH, PEER_NKEYS, PEER_DQ // 2), (PEER_DQ // 2) ** -0.5),
        'peer_k2': nrm((DEPTH, PEER_NKEYS, PEER_DQ // 2), (PEER_DQ // 2) ** -0.5),
        'peer_u': nrm((DEPTH, PEER_N, D_MODEL), inv),
        'peer_v': nrm((DEPTH, PEER_N, D_MODEL), (PEER_HEADS * PEER_TOPK) ** -0.5),
        'w_in_l0': nrm((D_MODEL, W_IN_DN), inv),
        'w_out_l0': nrm((DN_W + MEM_W, D_MODEL), (DN_W + MEM_W) ** -0.5),
        'dn_conv_l0': nrm((CONV_W, 3 * DN_W), CONV_W ** -0.5),
        'dn_alog_l0': a_log(DN_HEADS),
        'dn_dtb_l0': dt_bias(DN_HEADS),
        'dn_ng_l0': gain((DN_HD,)),
        'w_in_l1': nrm((D_MODEL, W_IN_FOX), inv),
        'w_out_l1': nrm((FOX_W + MEM_W, D_MODEL), (FOX_W + MEM_W) ** -0.5),
        'fox_bf_l1': 3.0 + nrm((FOX_HEADS,), 0.1),
        'w_in_l2': nrm((D_MODEL, W_IN_SSD), inv),
        'w_out_l2': nrm((SSD_DINNER + MEM_W, D_MODEL), (SSD_DINNER + MEM_W) ** -0.5),
        'ssd_conv_l2': nrm((CONV_W, SSD_CONV_DIM), CONV_W ** -0.5),
        'ssd_convb_l2': nrm((SSD_CONV_DIM,), 0.02),
        'ssd_alog_l2': a_log(SSD_HEADS),
        'ssd_dtb_l2': dt_bias(SSD_HEADS),
        'ssd_d_l2': 1.0 + nrm((SSD_HEADS,), 0.1),
        'ssd_ng_l2': gain((SSD_DINNER,)),
        'w_in_l3': nrm((D_MODEL, W_IN_DN), inv),
        'w_out_l3': nrm((DN_W + MEM_W, D_MODEL), (DN_W + MEM_W) ** -0.5),
        'dn_conv_l3': nrm((CONV_W, 3 * DN_W), CONV_W ** -0.5),
        'dn_alog_l3': a_log(DN_HEADS),
        'dn_dtb_l3': dt_bias(DN_HEADS),
        'dn_ng_l3': gain((DN_HD,)),
        'g_final': gain((D_MODEL,)),
    }


def reference(x_prompt, x_sample, cache_mem_k, cache_mem_v,
              state_l0_delta, state_l0_conv, cache_l1_k, cache_l1_v, cache_l1_logf,
              state_l2_ssm, state_l2_conv, state_l3_delta, state_l3_conv,
              mem_prompt,
              g_mix, g_mem, w_mem_k, w_mem_v, g_ffn, peer_wq, peer_k1, peer_k2, peer_u, peer_v,
              w_in_l0, w_out_l0, dn_conv_l0, dn_alog_l0, dn_dtb_l0, dn_ng_l0,
              w_in_l1, w_out_l1, fox_bf_l1,
              w_in_l2, w_out_l2, ssd_conv_l2, ssd_convb_l2, ssd_alog_l2, ssd_dtb_l2, ssd_d_l2, ssd_ng_l2,
              w_in_l3, w_out_l3, dn_conv_l3, dn_alog_l3, dn_dtb_l3, dn_ng_l3,
              g_final):
    mix_params = (
        {'w_in': w_in_l0, 'w_out': w_out_l0, 'conv': dn_conv_l0, 'alog': dn_alog_l0, 'dtb': dn_dtb_l0, 'ng': dn_ng_l0},
        {'w_in': w_in_l1, 'w_out': w_out_l1, 'bf': fox_bf_l1},
        {'w_in': w_in_l2, 'w_out': w_out_l2, 'conv': ssd_conv_l2, 'convb': ssd_convb_l2, 'alog': ssd_alog_l2, 'dtb': ssd_dtb_l2, 'd': ssd_d_l2, 'ng': ssd_ng_l2},
        {'w_in': w_in_l3, 'w_out': w_out_l3, 'conv': dn_conv_l3, 'alog': dn_alog_l3, 'dtb': dn_dtb_l3, 'ng': dn_ng_l3},
    )
    bp = x_prompt.shape[0]
    dtp = x_prompt.dtype
    prompt_states = (
        (jnp.zeros((bp, DN_HEADS, DN_HD, DN_HD), dtp), jnp.zeros((bp, CONV_W - 1, 3 * DN_W), dtp)),
        None,
        (jnp.zeros((bp, SSD_HEADS, SSD_HD, SSD_N), dtp), jnp.zeros((bp, CONV_W - 1, SSD_CONV_DIM), dtp)),
        (jnp.zeros((bp, DN_HEADS, DN_HD, DN_HD), dtp), jnp.zeros((bp, CONV_W - 1, 3 * DN_W), dtp)),
    )
    sample_states = (
        (state_l0_delta, state_l0_conv),
        (cache_l1_k, cache_l1_v, cache_l1_logf),
        (state_l2_ssm, state_l2_conv),
        (state_l3_delta, state_l3_conv),
    )

    x = x_prompt
    p_mk, p_mv, p_new = [], [], []
    for i in range(DEPTH):
        mn = rmsnorm(mem_prompt, g_mem[i])
        mk = (mn @ w_mem_k[i]).reshape(bp, N_MEM, MEM_HEADS, MEM_HD)
        mv = (mn @ w_mem_v[i]).reshape(bp, N_MEM, MEM_HEADS, MEM_HD)
        p_mk.append(mk)
        p_mv.append(mv)
        x, st = trunk_layer(MIXERS[i % len(MIXERS)], x, mk, mv, prompt_states[i], mix_params[i],
                            g_mix[i], g_ffn[i], peer_wq[i], peer_k1[i], peer_k2[i], peer_u[i], peer_v[i])
        p_new.append(st)
    y_prompt = rmsnorm(x, g_final)

    xs = x_sample
    s_new = []
    for i in range(DEPTH):
        xs, st = trunk_layer(MIXERS[i % len(MIXERS)], xs, cache_mem_k[i], cache_mem_v[i], sample_states[i],
                             mix_params[i], g_mix[i], g_ffn[i], peer_wq[i], peer_k1[i], peer_k2[i],
                             peer_u[i], peer_v[i])
        s_new.append(st)
    y_sample = rmsnorm(xs, g_final)

    p_mem_k = jnp.stack(p_mk)
    p_mem_v = jnp.stack(p_mv)
    p_l0_delta, p_l0_conv = p_new[0]
    p_l1_k, p_l1_v, p_l1_logf = p_new[1]
    p_l2_ssm, p_l2_conv = p_new[2]
    p_l3_delta, p_l3_conv = p_new[3]
    s_l0_delta, s_l0_conv = s_new[0]
    s_l1_k, s_l1_v, s_l1_logf = s_new[1]
    s_l2_ssm, s_l2_conv = s_new[2]
    s_l3_delta, s_l3_conv = s_new[3]
    return (y_prompt, y_sample, p_mem_k, p_mem_v,
            p_l0_delta, p_l0_conv, p_l1_k, p_l1_v, p_l1_logf, p_l2_ssm, p_l2_conv, p_l3_delta, p_l3_conv,
            s_l0_delta, s_l0_conv, s_l1_k, s_l1_v, s_l1_logf, s_l2_ssm, s_l2_conv, s_l3_delta, s_l3_conv)
```

```python
import functools

import jax
import jax.numpy as jnp
from jax import lax
from jax.experimental import pallas as pl
from jax.experimental.pallas import tpu as pltpu

F32 = jnp.float32
BF16 = jnp.bfloat16

EPS = 1e-6
CONV_W = 4
CHUNK = 64
DN_HEADS, DN_HD = 8, 128
DN_W = DN_HEADS * DN_HD
FOX_HEADS, FOX_HD = 16, 64
FOX_W = FOX_HEADS * FOX_HD
SSD_DINNER, SSD_HD, SSD_GROUPS, SSD_N = 2048, 64, 4, 128
SSD_HEADS = SSD_DINNER // SSD_HD
SSD_HPG = SSD_HEADS // SSD_GROUPS
SSD_CONV_DIM = SSD_DINNER + 2 * SSD_GROUPS * SSD_N
N_MEM, MEM_HEADS, MEM_HD = 256, 4, 256
MEM_W = MEM_HEADS * MEM_HD
PEER_HEADS, PEER_NKEYS, PEER_DQ, PEER_TOPK = 8, 128, 256, 16
PEER_HALF = PEER_DQ // 2

LANE = 128
SUBLANE = 8
VMEM_LIMIT = 56 * 1024 * 1024
NEG_BIG = -1e30
NEG_MAX = -3.0e38
SQRT_HALF = 0.7071067811865476

_NT = (((1,), (1,)), ((), ()))
_TN = (((0,), (0,)), ((), ()))


def _cparams(sem):
    return pltpu.CompilerParams(dimension_semantics=sem, vmem_limit_bytes=VMEM_LIMIT)


def _pick(n, pref):
    for t in range(min(n, pref), 0, -1):
        if n % t == 0 and t % SUBLANE == 0:
            return t
    raise ValueError(f"no tile for {n}")


def _dot(a, b):
    return jnp.dot(a, b, preferred_element_type=F32)


def _dot_nt(a, b):
    return lax.dot_general(a, b, _NT, preferred_element_type=F32)


def _dot_tn(a, b):
    return lax.dot_general(a, b, _TN, preferred_element_type=F32)


def _split3(x):
    hi = x.astype(BF16)
    r1 = x - hi.astype(F32)
    mid = r1.astype(BF16)
    lo = (r1 - mid.astype(F32)).astype(BF16)
    return hi, mid, lo


def _dot_exact_l(mask_bf16, x):
    hi, mid, lo = _split3(x)
    return _dot(mask_bf16, hi) + _dot(mask_bf16, mid) + _dot(mask_bf16, lo)


def _dot_exact_r(x, mask_bf16):
    hi, mid, lo = _split3(x)
    return _dot(hi, mask_bf16) + _dot(mid, mask_bf16) + _dot(lo, mask_bf16)


def _sigmoid(x):
    return 1.0 / (1.0 + jnp.exp(-x))


def _silu(x):
    return x * _sigmoid(x)


def _softplus(x):
    return jnp.maximum(x, 0.0) + jnp.log1p(jnp.exp(-jnp.abs(x)))


def _rms(x, g):
    ms = jnp.mean(x * x, axis=-1, keepdims=True)
    return x * lax.rsqrt(ms + EPS) * g


def _tri_masks(c):
    row = lax.broadcasted_iota(jnp.int32, (c, c), 0)
    col = lax.broadcasted_iota(jnp.int32, (c, c), 1)
    return row >= col, row > col


def _norm_matmul_kernel(x_ref, g_ref, w_ref, ws_ref, o_ref, os_ref, xn_ref):
    @pl.when(pl.program_id(1) == 0)
    def _():
        xn = _rms(x_ref[...], g_ref[...]).astype(BF16)
        xn_ref[...] = xn
        os_ref[...] = _dot(xn, ws_ref[...])
    o_ref[...] = _dot(xn_ref[...], w_ref[...])


def _norm_matmul(x2d, g, w_main, w_small):
    t, d = x2d.shape
    n = w_main.shape[1]
    tm = _pick(t, 512)
    tn = 1024
    assert n % tn == 0 and w_small.shape == (d, LANE)
    return pl.pallas_call(
        _norm_matmul_kernel,
        grid=(t // tm, n // tn),
        in_specs=[
            pl.BlockSpec((tm, d), lambda i, j: (i, 0)),
            pl.BlockSpec((1, d), lambda i, j: (0, 0)),
            pl.BlockSpec((d, tn), lambda i, j: (0, j)),
            pl.BlockSpec((d, LANE), lambda i, j: (0, 0)),
        ],
        out_specs=[
            pl.BlockSpec((tm, tn), lambda i, j: (i, j)),
            pl.BlockSpec((tm, LANE), lambda i, j: (i, 0)),
        ],
        out_shape=[jax.ShapeDtypeStruct((t, n), F32), jax.ShapeDtypeStruct((t, LANE), F32)],
        scratch_shapes=[pltpu.VMEM((tm, d), BF16)],
        compiler_params=_cparams(("parallel", "arbitrary")),
        name="norm_matmul",
    )(x2d, g.reshape(1, d), w_main, w_small)


def _conv_kernel(u_ref, prev_ref, head_ref, w_ref, b_ref, o_ref, xcat_ref, *, tl):
    i = pl.program_id(1)

    @pl.when(i == 0)
    def _():
        xcat_ref[0:SUBLANE, :] = head_ref[0]

    @pl.when(i > 0)
    def _():
        xcat_ref[0:SUBLANE, :] = prev_ref[0]

    xcat_ref[SUBLANE:SUBLANE + tl, :] = u_ref[0]
    base = SUBLANE - (CONV_W - 1)
    acc = xcat_ref[base:base + tl, :] * w_ref[0:1, :]
    for j in range(1, CONV_W):
        acc = acc + xcat_ref[base + j:base + j + tl, :] * w_ref[j:j + 1, :]
    acc = acc + b_ref[...]
    o_ref[0] = _silu(acc)


def _conv_silu(proj, col0, head8, w, bias):
    b, l, _ = proj.shape
    width = w.shape[1]
    tc = 1024
    tl = _pick(l, 512)
    assert col0 % tc == 0 and width % tc == 0 and tl % SUBLANE == 0
    cb0 = col0 // tc
    rb = tl // SUBLANE
    return pl.pallas_call(
        functools.partial(_conv_kernel, tl=tl),
        grid=(b, l // tl, width // tc),
        in_specs=[
            pl.BlockSpec((1, tl, tc), lambda bi, i, j: (bi, i, cb0 + j)),
            pl.BlockSpec((1, SUBLANE, tc), lambda bi, i, j: (bi, jnp.maximum(i * rb - 1, 0), cb0 + j)),
            pl.BlockSpec((1, SUBLANE, tc), lambda bi, i, j: (bi, 0, j)),
            pl.BlockSpec((CONV_W, tc), lambda bi, i, j: (0, j)),
            pl.BlockSpec((1, tc), lambda bi, i, j: (0, j)),
        ],
        out_specs=pl.BlockSpec((1, tl, tc), lambda bi, i, j: (bi, i, j)),
        out_shape=jax.ShapeDtypeStruct((b, l, width), F32),
        scratch_shapes=[pltpu.VMEM((SUBLANE + tl, tc), F32)],
        compiler_params=_cparams(("parallel", "parallel", "parallel")),
        name="conv_silu",
    )(proj, proj, head8, w, bias.reshape(1, width))


def _delta_kernel(alog_ref, dtb_ref, q_ref, k_ref, v_ref, z_ref, ab_ref, ng_ref, s0_ref,
                  o_ref, sout_ref, s_ref, *, c, nsq):
    h = pl.program_id(1)
    ci = pl.program_id(2)

    @pl.when(ci == 0)
    def _():
        s_ref[...] = s0_ref[0, 0]

    q = q_ref[0]
    k = k_ref[0]
    v = v_ref[0]
    q = q * lax.rsqrt(jnp.sum(q * q, axis=-1, keepdims=True) + EPS) * (DN_HD ** -0.5)
    k = k * lax.rsqrt(jnp.sum(k * k, axis=-1, keepdims=True) + EPS)

    ab = ab_ref[0]
    lane = lax.broadcasted_iota(jnp.int32, ab.shape, 1)
    a_col = jnp.sum(jnp.where(lane == h, ab, 0.0), axis=1, keepdims=True)
    b_col = jnp.sum(jnp.where(lane == h + DN_HEADS, ab, 0.0), axis=1, keepdims=True)
    neg_a = -jnp.exp(jnp.zeros((c, 1), F32) + alog_ref[h])
    g_col = neg_a * _softplus(a_col + dtb_ref[h])
    beta = _sigmoid(b_col)

    tri, strict = _tri_masks(c)
    ltri = jnp.where(tri, 1.0, 0.0).astype(BF16)
    declog = _dot_exact_l(ltri, g_col * jnp.where(strict, 1.0, 0.0))
    gamc = _dot_exact_l(ltri, jnp.broadcast_to(g_col, (c, DN_HD)))
    dec = jnp.where(tri, jnp.exp(declog), 0.0)
    eg = jnp.exp(gamc)

    s = s_ref[...]
    kb = k.astype(BF16)
    qb = q.astype(BF16)
    sb = s.astype(BF16)
    kk = _dot_nt(kb, kb)
    a_mat = jnp.where(strict, beta * dec * kk, 0.0)
    rhs = beta * (v - eg * _dot(kb, sb))

    eye = jnp.where(tri, 1.0, 0.0) - jnp.where(strict, 1.0, 0.0)
    p = -a_mat
    t_inv = eye + p
    for _ in range(nsq):
        p = _mm_hi(p, p)
        t_inv = t_inv + _mm_hi(t_inv, p)
    delta = _mm_hi(t_inv, rhs)

    qk = dec * _dot_nt(qb, kb)
    db = delta.astype(BF16)
    o = eg * _dot(qb, sb) + _dot(qk.astype(BF16), db)
    g_last = gamc[c - 1:c, :]
    kw = k * jnp.exp(g_last - gamc)
    s_new = jnp.exp(g_last) * s + _dot_tn(kw.astype(BF16), db)
    s_ref[...] = s_new

    on = _rms(o, ng_ref[...])
    o_ref[0] = (on * _silu(z_ref[0])).astype(BF16)

    @pl.when(ci == pl.num_programs(2) - 1)
    def _():
        sout_ref[0, 0] = s_new


def _mm_hi(a, b):
    a_hi = a.astype(BF16)
    b_hi = b.astype(BF16)
    a_lo = (a - a_hi.astype(F32)).astype(BF16)
    b_lo = (b - b_hi.astype(F32)).astype(BF16)
    return _dot(a_hi, b_hi) + _dot(a_hi, b_lo) + _dot(a_lo, b_hi)


def _delta_scan(qkv, proj, small, alog, dtb, ng, s0):
    b, l, _ = qkv.shape
    c = min(CHUNK, l)
    assert l % c == 0 and c % 16 == 0
    nsq = max(c.bit_length() - 2, 0)
    zb = (3 * DN_W) // DN_HD
    smem = pl.BlockSpec(memory_space=pltpu.SMEM)
    blk = lambda off: pl.BlockSpec((1, c, DN_HD), lambda bi, h, ci: (bi, ci, off + h))
    return pl.pallas_call(
        functools.partial(_delta_kernel, c=c, nsq=nsq),
        grid=(b, DN_HEADS, l // c),
        in_specs=[
            smem, smem,
            blk(0), blk(DN_HEADS), blk(2 * DN_HEADS),
            blk(zb),
            pl.BlockSpec((1, c, LANE), lambda bi, h, ci: (bi, ci, 0)),
            pl.BlockSpec((1, DN_HD), lambda bi, h, ci: (0, 0)),
            pl.BlockSpec((1, 1, DN_HD, DN_HD), lambda bi, h, ci: (bi, h, 0, 0)),
        ],
        out_specs=[
            pl.BlockSpec((1, c, DN_HD), lambda bi, h, ci: (bi, ci, h)),
            pl.BlockSpec((1, 1, DN_HD, DN_HD), lambda bi, h, ci: (bi, h, 0, 0)),
        ],
        out_shape=[jax.ShapeDtypeStruct((b, l, DN_W), BF16),
                   jax.ShapeDtypeStruct((b, DN_HEADS, DN_HD, DN_HD), F32)],
        scratch_shapes=[pltpu.VMEM((DN_HD, DN_HD), F32)],
        compiler_params=_cparams(("parallel", "parallel", "arbitrary")),
        name="delta_scan",
    )(alog, dtb, qkv, qkv, qkv, proj, small, ng.reshape(1, DN_HD), s0)


def _fox_gate_kernel(z_ref, bias_ref, logf_ref, cum_ref, carry_ref, *, blk, past):
    i = pl.program_id(0)

    @pl.when(i == 0)
    def _():
        carry_ref[...] = jnp.zeros_like(carry_ref)

    z = z_ref[...]
    row = i * blk + lax.broadcasted_iota(jnp.int32, z.shape, 0)
    zz = z + bias_ref[...]
    log_sig = jnp.minimum(zz, 0.0) - jnp.log1p(jnp.exp(-jnp.abs(zz)))
    logf = jnp.where(row >= past, log_sig, z)
    logf_ref[...] = logf
    tri, _ = _tri_masks(blk)
    cum = _dot_exact_l(jnp.where(tri, 1.0, 0.0).astype(BF16), logf) + carry_ref[...]
    cum_ref[...] = cum
    carry_ref[...] = cum[blk - 1:blk, :]


def _fox_gate(zall, bias_row, past):
    s, w = zall.shape
    blk = _pick(s, 512)
    return pl.pallas_call(
        functools.partial(_fox_gate_kernel, blk=blk, past=past),
        grid=(s // blk,),
        in_specs=[pl.BlockSpec((blk, w), lambda i: (i, 0)), pl.BlockSpec((1, w), lambda i: (0, 0))],
        out_specs=[pl.BlockSpec((blk, w), lambda i: (i, 0)), pl.BlockSpec((blk, w), lambda i: (i, 0))],
        out_shape=[jax.ShapeDtypeStruct((s, w), F32), jax.ShapeDtypeStruct((s, w), F32)],
        scratch_shapes=[pltpu.VMEM((1, w), F32)],
        compiler_params=_cparams(("arbitrary",)),
        name="fox_gate",
    )(zall, bias_row)


def _fox_attn_kernel(q_ref, k_ref, v_ref, cq_ref, ck_ref, o_ref, m_ref, l_ref, acc_ref, *, t):
    hp = pl.program_id(1)
    qi = pl.program_id(2)
    ki = pl.program_id(3)

    @pl.when(ki == 0)
    def _():
        m_ref[...] = jnp.full_like(m_ref, NEG_BIG)
        l_ref[...] = jnp.zeros_like(l_ref)
        acc_ref[...] = jnp.zeros_like(acc_ref)

    @pl.when(ki <= qi)
    def _():
        q = q_ref[0]
        k = k_ref[0]
        v = v_ref[0]
        cqb = cq_ref[0]
        lane = lax.broadcasted_iota(jnp.int32, cqb.shape, 1)
        qpos = qi * t + lax.broadcasted_iota(jnp.int32, (t, t), 0)
        kpos = ki * t + lax.broadcasted_iota(jnp.int32, (t, t), 1)
        mask = kpos <= qpos
        for hh in range(2):
            h = hp * 2 + hh
            sl = slice(hh * FOX_HD, (hh + 1) * FOX_HD)
            s = _dot_nt(q[:, sl].astype(BF16), k[:, sl].astype(BF16)) * (FOX_HD ** -0.5)
            cq = jnp.sum(jnp.where(lane == h, cqb, 0.0), axis=1, keepdims=True)
            ck = ck_ref[0, pl.ds(h, 1), :]
            s = jnp.where(mask, s + cq - ck, NEG_BIG)
            m_prev = m_ref[hh]
            m_new = jnp.maximum(m_prev, jnp.max(s, axis=1, keepdims=True))
            alpha = jnp.exp(m_prev - m_new)
            p = jnp.exp(s - m_new)
            l_ref[hh] = alpha * l_ref[hh] + jnp.sum(p, axis=1, keepdims=True)
            acc_ref[hh] = alpha * acc_ref[hh] + _dot(p.astype(BF16), v[:, sl].astype(BF16))
            m_ref[hh] = m_new

    @pl.when(ki == pl.num_programs(3) - 1)
    def _():
        o = jnp.concatenate([acc_ref[0] / l_ref[0], acc_ref[1] / l_ref[1]], axis=1)
        o_ref[0] = o.astype(BF16)


def _fox_attn(proj, cum_bl, cum_bt):
    b, l, _ = proj.shape
    t = _pick(l, 512)
    n = l // t
    hpn = FOX_HEADS // 2
    kv = lambda off: pl.BlockSpec((1, t, LANE), lambda bi, hp, qi, ki: (bi, jnp.minimum(ki, qi), off + hp))
    return pl.pallas_call(
        functools.partial(_fox_attn_kernel, t=t),
        grid=(b, hpn, n, n),
        in_specs=[
            pl.BlockSpec((1, t, LANE), lambda bi, hp, qi, ki: (bi, qi, hp)),
            kv(hpn), kv(2 * hpn),
            pl.BlockSpec((1, t, FOX_HEADS), lambda bi, hp, qi, ki: (bi, qi, 0)),
            pl.BlockSpec((1, FOX_HEADS, t), lambda bi, hp, qi, ki: (bi, 0, jnp.minimum(ki, qi))),
        ],
        out_specs=pl.BlockSpec((1, t, LANE), lambda bi, hp, qi, ki: (bi, qi, hp)),
        out_shape=jax.ShapeDtypeStruct((b, l, FOX_W), BF16),
        scratch_shapes=[pltpu.VMEM((2, t, 1), F32), pltpu.VMEM((2, t, 1), F32), pltpu.VMEM((2, t, FOX_HD), F32)],
        compiler_params=_cparams(("parallel", "parallel", "parallel", "arbitrary")),
        name="fox_attn",
    )(proj, proj, proj, cum_bl, cum_bt)


def _fox_dec_kernel(q_ref, kn_ref, vn_ref, kc_ref, vc_ref, cq_ref, ckp_ref, ckn_ref, o_ref,
                    qbd_ref, m_ref, l_ref, acc_ref, *, l):
    ki = pl.program_id(1)
    rows = FOX_HEADS * l

    @pl.when(ki == 0)
    def _():
        q = q_ref[0]
        lane = lax.broadcasted_iota(jnp.int32, q.shape, 1)
        for h in range(FOX_HEADS):
            keep = jnp.where(lane >= h * FOX_HD, jnp.where(lane < (h + 1) * FOX_HD, q, 0.0), 0.0)
            qbd_ref[h * l:(h + 1) * l, :] = keep.astype(BF16)
        m_ref[...] = jnp.full_like(m_ref, NEG_BIG)
        l_ref[...] = jnp.zeros_like(l_ref)
        acc_ref[...] = jnp.zeros_like(acc_ref)

    cqb = cq_ref[0]
    cq = jnp.concatenate([cqb[:, h:h + 1] for h in range(FOX_HEADS)], axis=0)

    def step(kblk, vblk, ckt, causal):
        n = kblk.shape[0]
        s = _dot_nt(qbd_ref[...], kblk.astype(BF16)) * (FOX_HD ** -0.5)
        ckb = jnp.concatenate([jnp.broadcast_to(ckt[h:h + 1, :], (l, n)) for h in range(FOX_HEADS)], axis=0)
        s = s + cq - ckb
        if causal:
            tq = jnp.concatenate([lax.broadcasted_iota(jnp.int32, (l, n), 0)] * FOX_HEADS, axis=0)
            tk = lax.broadcasted_iota(jnp.int32, (rows, n), 1)
            s = jnp.where(tk <= tq, s, NEG_BIG)
        m_prev = m_ref[...]
        m_new = jnp.maximum(m_prev, jnp.max(s, axis=1, keepdims=True))
        alpha = jnp.exp(m_prev - m_new)
        p = jnp.exp(s - m_new)
        l_ref[...] = alpha * l_ref[...] + jnp.sum(p, axis=1, keepdims=True)
        acc_ref[...] = alpha * acc_ref[...] + _dot(p.astype(BF16), vblk.astype(BF16))
        m_ref[...] = m_new

    step(kc_ref[0], vc_ref[0], ckp_ref[0], False)

    @pl.when(ki == pl.num_programs(1) - 1)
    def _():
        step(kn_ref[0], vn_ref[0], ckn_ref[0], True)
        o = acc_ref[...] / l_ref[...]
        o_ref[0] = jnp.concatenate(
            [o[h * l:(h + 1) * l, h * FOX_HD:(h + 1) * FOX_HD] for h in range(FOX_HEADS)], axis=1).astype(BF16)


def _fox_dec(proj, kc, vc, cq, ckp, ckn):
    b, l, _ = proj.shape
    p = kc.shape[1]
    tk = _pick(p, 1024)
    assert tk % LANE == 0 or tk == p
    rows = FOX_HEADS * l
    pj = lambda j: pl.BlockSpec((1, l, FOX_W), lambda bi, ki: (bi, 0, j))
    return pl.pallas_call(
        functools.partial(_fox_dec_kernel, l=l),
        grid=(b, p // tk),
        in_specs=[
            pj(0), pj(1), pj(2),
            pl.BlockSpec((1, tk, FOX_W), lambda bi, ki: (bi, ki, 0)),
            pl.BlockSpec((1, tk, FOX_W), lambda bi, ki: (bi, ki, 0)),
            pl.BlockSpec((1, l, FOX_HEADS), lambda bi, ki: (bi, 0, 0)),
            pl.BlockSpec((1, FOX_HEADS, tk), lambda bi, ki: (bi, 0, ki)),
            pl.BlockSpec((1, FOX_HEADS, l), lambda bi, ki: (bi, 0, 0)),
        ],
        out_specs=pl.BlockSpec((1, l, FOX_W), lambda bi, ki: (bi, 0, 0)),
        out_shape=jax.ShapeDtypeStruct((b, l, FOX_W), BF16),
        scratch_shapes=[pltpu.VMEM((rows, FOX_W), BF16), pltpu.VMEM((rows, 1), F32),
                        pltpu.VMEM((rows, 1), F32), pltpu.VMEM((rows, FOX_W), F32)],
        compiler_params=_cparams(("parallel", "arbitrary")),
        name="fox_dec",
    )(proj, proj, proj, kc, vc, cq, ckp, ckn)


def _ssd_kernel(alog_r_ref, dtb_r_ref, dsk_r_ref, alog_c_ref, dtb_c_ref,
                xs_ref, bm_ref, cm_ref, z_ref, dtc_ref, dtr_ref, h0_ref,
                y_ref, hout_ref, h_ref, *, c):
    ci = pl.program_id(2)

    @pl.when(ci == 0)
    def _():
        h_ref[...] = h0_ref[0]

    dt_c = _softplus(dtc_ref[0, 0] + dtb_r_ref[0])
    da_c = dt_c * (-jnp.exp(alog_r_ref[0]))
    dt_r = _softplus(dtr_ref[0, 0, 0] + dtb_c_ref[0])
    da_r = dt_r * (-jnp.exp(alog_c_ref[0]))
    tri, _ = _tri_masks(c)
    trif = jnp.where(tri, 1.0, 0.0)
    gam_c = _dot_exact_l(trif.astype(BF16), da_c)
    row = lax.broadcasted_iota(jnp.int32, (c, c), 0)
    col = lax.broadcasted_iota(jnp.int32, (c, c), 1)
    gam_r = _dot_exact_r(da_r, jnp.where(row <= col, 1.0, 0.0).astype(BF16))

    bmb = bm_ref[0].astype(BF16)
    cmb = cm_ref[0].astype(BF16)
    cb = _dot_nt(cmb, bmb)
    xs = xs_ref[0]
    dsk = dsk_r_ref[0]
    ys = []
    for j in range(SSD_HPG):
        xj = xs[:, j * SSD_HD:(j + 1) * SSD_HD]
        gj = gam_c[:, j:j + 1]
        dec = jnp.where(tri, jnp.exp(gj - gam_r[j:j + 1, :]), 0.0)
        xdt = xj * dt_c[:, j:j + 1]
        hj = h_ref[j]
        y = _dot((dec * cb).astype(BF16), xdt.astype(BF16)) + jnp.exp(gj) * _dot_nt(cmb, hj.astype(BF16))
        gl = gam_c[c - 1:c, j:j + 1]
        xw = xdt * jnp.exp(gl - gj)
        h_ref[j] = jnp.exp(gl) * hj + _dot_tn(xw.astype(BF16), bmb)
        ys.append(y + dsk[:, j:j + 1] * xj)
    y_ref[0] = jnp.concatenate(ys, axis=1) * _silu(z_ref[0])

    @pl.when(ci == pl.num_programs(2) - 1)
    def _():
        hout_ref[0] = h_ref[...]


def _ssd_scan(xbc, proj, small, alog, dtb, dskip, h0):
    b, l, _ = xbc.shape
    c = min(CHUNK, l)
    assert l % c == 0
    gw = SSD_HPG * SSD_HD
    nb = SSD_DINNER // SSD_N
    dt = small[..., :SSD_HEADS].reshape(b, l, SSD_GROUPS, SSD_HPG)
    dtc = dt.transpose(0, 2, 1, 3)
    dtr = dt.reshape(b, l // c, c, SSD_GROUPS, SSD_HPG).transpose(0, 3, 1, 4, 2)
    prow = lambda a: a.reshape(SSD_GROUPS, 1, SSD_HPG)
    pcol = lambda a: a.reshape(SSD_GROUPS, SSD_HPG, 1)
    prs = pl.BlockSpec((1, 1, SSD_HPG), lambda bi, g, ci: (g, 0, 0))
    pcs = pl.BlockSpec((1, SSD_HPG, 1), lambda bi, g, ci: (g, 0, 0))
    return pl.pallas_call(
        functools.partial(_ssd_kernel, c=c),
        grid=(b, SSD_GROUPS, l // c),
        in_specs=[
            prs, prs, prs, pcs, pcs,
            pl.BlockSpec((1, c, gw), lambda bi, g, ci: (bi, ci, g)),
            pl.BlockSpec((1, c, SSD_N), lambda bi, g, ci: (bi, ci, nb + g)),
            pl.BlockSpec((1, c, SSD_N), lambda bi, g, ci: (bi, ci, nb + SSD_GROUPS + g)),
            pl.BlockSpec((1, c, gw), lambda bi, g, ci: (bi, ci, g)),
            pl.BlockSpec((1, 1, c, SSD_HPG), lambda bi, g, ci: (bi, g, ci, 0)),
            pl.BlockSpec((1, 1, 1, SSD_HPG, c), lambda bi, g, ci: (bi, g, ci, 0, 0)),
            pl.BlockSpec((1, SSD_HPG, SSD_HD, SSD_N), lambda bi, g, ci: (bi, g, 0, 0)),
        ],
        out_specs=[
            pl.BlockSpec((1, c, gw), lambda bi, g, ci: (bi, ci, g)),
            pl.BlockSpec((1, SSD_HPG, SSD_HD, SSD_N), lambda bi, g, ci: (bi, g, 0, 0)),
        ],
        out_shape=[jax.ShapeDtypeStruct((b, l, SSD_DINNER), F32),
                   jax.ShapeDtypeStruct((b, SSD_HEADS, SSD_HD, SSD_N), F32)],
        scratch_shapes=[pltpu.VMEM((SSD_HPG, SSD_HD, SSD_N), F32)],
        compiler_params=_cparams(("parallel", "parallel", "arbitrary")),
        name="ssd_scan",
    )(prow(alog), prow(dtb), prow(dskip), pcol(alog), pcol(dtb), xbc, xbc, xbc, proj, dtc, dtr, h0)


def _mem_attn_kernel(q_ref, mk_ref, mv_ref, o_ref):
    q = q_ref[0]
    outs = []
    for h in range(MEM_HEADS):
        sl = slice(h * MEM_HD, (h + 1) * MEM_HD)
        s = _dot_nt(q[:, sl].astype(BF16), mk_ref[0, :, sl].astype(BF16)) * (MEM_HD ** -0.5)
        m = jnp.max(s, axis=1, keepdims=True)
        p = jnp.exp(s - m)
        lsum = jnp.sum(p, axis=1, keepdims=True)
        outs.append(_dot(p.astype(BF16), mv_ref[0, :, sl].astype(BF16)) / lsum)
    o_ref[0] = jnp.concatenate(outs, axis=1).astype(BF16)


def _mem_attn(proj, qblk, mk, kblk, mv, vblk):
    b, l, _ = proj.shape
    tq = _pick(l, 512)
    return pl.pallas_call(
        _mem_attn_kernel,
        grid=(b, l // tq),
        in_specs=[
            pl.BlockSpec((1, tq, MEM_W), lambda bi, i: (bi, i, qblk)),
            pl.BlockSpec((1, N_MEM, MEM_W), lambda bi, i: (bi, 0, kblk)),
            pl.BlockSpec((1, N_MEM, MEM_W), lambda bi, i: (bi, 0, vblk)),
        ],
        out_specs=pl.BlockSpec((1, tq, MEM_W), lambda bi, i: (bi, i, 0)),
        out_shape=jax.ShapeDtypeStruct((b, l, MEM_W), BF16),
        compiler_params=_cparams(("parallel", "parallel")),
        name="mem_attn",
    )(proj, mk, mv)


def _out_proj_kernel(a1_ref, a2_ref, w_ref, x_ref, ng_ref, o_ref, *, w1, norm):
    a1 = a1_ref[...]
    if norm:
        a1 = _rms(a1, ng_ref[...]).astype(BF16)
    y = _dot(a1, w_ref[0:w1, :]) + _dot(a2_ref[...], w_ref[w1:, :])
    o_ref[...] = x_ref[...] + y


def _out_proj(a1, a2, w, x2d, ng=None):
    t, d = x2d.shape
    w1 = a1.shape[1]
    w2 = a2.shape[1]
    tm = _pick(t, 512)
    norm = ng is not None
    if ng is None:
        ng = jnp.ones((w1,), F32)
    return pl.pallas_call(
        functools.partial(_out_proj_kernel, w1=w1, norm=norm),
        grid=(t // tm,),
        in_specs=[
            pl.BlockSpec((tm, w1), lambda i: (i, 0)),
            pl.BlockSpec((tm, w2), lambda i: (i, 0)),
            pl.BlockSpec((w1 + w2, d), lambda i: (0, 0)),
            pl.BlockSpec((tm, d), lambda i: (i, 0)),
            pl.BlockSpec((1, w1), lambda i: (0, 0)),
        ],
        out_specs=pl.BlockSpec((tm, d), lambda i: (i, 0)),
        out_shape=jax.ShapeDtypeStruct((t, d), F32),
        compiler_params=_cparams(("parallel",)),
        name="out_proj",
    )(a1, a2, w, x2d, ng.reshape(1, w1))


def _peer_query_kernel(x_ref, g_ref, wq_ref, k1_ref, k2_ref, xn_ref, s1_ref, s2_ref):
    xn = _rms(x_ref[...], g_ref[...]).astype(BF16)
    xn_ref[...] = xn
    qb = _dot(xn, wq_ref[...]).astype(BF16)
    for h in range(PEER_HEADS):
        o = h * PEER_DQ
        s1_ref[h] = _dot_nt(k1_ref[...], qb[:, o:o + PEER_HALF])
        s2_ref[h] = _dot_nt(k2_ref[...], qb[:, o + PEER_HALF:o + PEER_DQ])


def _peer_query(x2d, g, wq, k1, k2):
    t, d = x2d.shape
    tm = _pick(t, 256)
    nq = PEER_HEADS * PEER_DQ
    sshape = jax.ShapeDtypeStruct((PEER_HEADS, PEER_NKEYS, t), F32)
    sspec = pl.BlockSpec((PEER_HEADS, PEER_NKEYS, tm), lambda i: (0, 0, i))
    return pl.pallas_call(
        _peer_query_kernel,
        grid=(t // tm,),
        in_specs=[
            pl.BlockSpec((tm, d), lambda i: (i, 0)),
            pl.BlockSpec((1, d), lambda i: (0, 0)),
            pl.BlockSpec((d, nq), lambda i: (0, 0)),
            pl.BlockSpec((PEER_NKEYS, PEER_HALF), lambda i: (0, 0)),
            pl.BlockSpec((PEER_NKEYS, PEER_HALF), lambda i: (0, 0)),
        ],
        out_specs=[pl.BlockSpec((tm, d), lambda i: (i, 0)), sspec, sspec],
        out_shape=[jax.ShapeDtypeStruct((t, d), BF16), sshape, sshape],
        compiler_params=_cparams(("parallel",)),
        name="peer_query",
    )(x2d, g.reshape(1, d), wq, k1, k2)


_N_EXT = PEER_TOPK + 1
_CAND_ROWS = [_N_EXT // (r + 1) for r in range(_N_EXT)]
_N_CAND = sum(_CAND_ROWS)
_CAND_PAD = -(-_N_CAND // SUBLANE) * SUBLANE
_EXT_PAD = -(-_N_EXT // SUBLANE) * SUBLANE


def _extract_desc(x, out_ref, n):
    def body(r, xc):
        m = jnp.max(xc, axis=0, keepdims=True)
        out_ref[pl.ds(r, 1), :] = m
        return jnp.where(xc == m, NEG_MAX, xc)
    lax.fori_loop(0, n, body, x)


def _peer_topk_kernel(s1_ref, s2_ref, st_ref, v1_ref, v2_ref, cand_ref, c_ref):
    def per_head(h, carry):
        _extract_desc(s1_ref[h], v1_ref, _N_EXT)
        _extract_desc(s2_ref[h], v2_ref, _N_EXT)
        off = 0
        for r1, n2 in enumerate(_CAND_ROWS):
            cand_ref[off:off + n2, :] = v1_ref[r1:r1 + 1, :] + v2_ref[0:n2, :]
            off += n2
        if _CAND_PAD > _N_CAND:
            cand_ref[_N_CAND:_CAND_PAD, :] = jnp.full((_CAND_PAD - _N_CAND, cand_ref.shape[1]), NEG_MAX, F32)
        _extract_desc(cand_ref[...], c_ref, _N_EXT)
        top = c_ref[0:PEER_TOPK, :]
        zsum = jnp.sum(jnp.exp(top - top[0:1, :]), axis=0, keepdims=True)
        theta = 0.5 * (c_ref[PEER_TOPK - 1:PEER_TOPK, :] + c_ref[PEER_TOPK:PEER_TOPK + 1, :])
        st_ref[h, 0:1, :] = theta
        st_ref[h, 1:2, :] = v1_ref[0:1, :]
        st_ref[h, 2:3, :] = v2_ref[0:1, :]
        st_ref[h, 3:4, :] = 1.0 / zsum
        st_ref[h, 4:SUBLANE, :] = jnp.zeros((SUBLANE - 4, st_ref.shape[2]), F32)
        return carry
    lax.fori_loop(0, PEER_HEADS, per_head, 0)


def _peer_topk(s1, s2):
    t = s1.shape[2]
    tb = _pick(t, 256)
    sspec = pl.BlockSpec((PEER_HEADS, PEER_NKEYS, tb), lambda i: (0, 0, i))
    return pl.pallas_call(
        _peer_topk_kernel,
        grid=(t // tb,),
        in_specs=[sspec, sspec],
        out_specs=pl.BlockSpec((PEER_HEADS, SUBLANE, tb), lambda i: (0, 0, i)),
        out_shape=jax.ShapeDtypeStruct((PEER_HEADS, SUBLANE, t), F32),
        scratch_shapes=[pltpu.VMEM((_EXT_PAD, tb), F32), pltpu.VMEM((_EXT_PAD, tb), F32),
                        pltpu.VMEM((_CAND_PAD, tb), F32), pltpu.VMEM((_EXT_PAD, tb), F32)],
        compiler_params=_cparams(("parallel",)),
        name="peer_topk",
    )(s1, s2)


_EC = 8


def _peer_expert_kernel(xn_ref, u_ref, v_ref, s1_ref, s2_ref, st_ref, x_ref, o_ref,
                        thr_ref, e1_ref, e2_ref, h_ref, g_ref, acc_ref):
    ci = pl.program_id(1)

    @pl.when(ci == 0)
    def _():
        for h in range(PEER_HEADS):
            st = st_ref[h]
            s1 = s1_ref[h]
            thr_ref[h] = st[0:1, :] - s1
            e1_ref[h] = jnp.exp(s1 - st[1:2, :])
            e2_ref[h] = jnp.exp(s2_ref[h] - st[2:3, :]) * st[3:4, :]
        acc_ref[...] = jnp.zeros_like(acc_ref)

    h_ref[...] = _dot_nt(u_ref[...], xn_ref[...])
    for j in range(_EC):
        i1 = ci * _EC + j
        w = None
        for h in range(PEER_HEADS):
            thr_row = thr_ref[h, pl.ds(i1, 1), :]
            e1_row = e1_ref[h, pl.ds(i1, 1), :]
            contrib = jnp.where(s2_ref[h] >= thr_row, e2_ref[h] * e1_row, 0.0)
            w = contrib if w is None else w + contrib
        hj = h_ref[j * PEER_NKEYS:(j + 1) * PEER_NKEYS, :]
        act = 0.5 * hj * (1.0 + lax.erf(hj * SQRT_HALF))
        g_ref[j * PEER_NKEYS:(j + 1) * PEER_NKEYS, :] = (w * act).astype(BF16)
    acc_ref[...] += _dot_tn(g_ref[...], v_ref[...])

    @pl.when(ci == pl.num_programs(1) - 1)
    def _():
        o_ref[...] = x_ref[...] + acc_ref[...]


def _peer_expert(xn, u, v, s1, s2, st, x2d):
    t, d = x2d.shape
    tb = _pick(t, 256)
    ne = u.shape[0]
    ec = _EC * PEER_NKEYS
    assert ne == PEER_NKEYS * PEER_NKEYS and ne % ec == 0
    sspec = pl.BlockSpec((PEER_HEADS, PEER_NKEYS, tb), lambda i, c: (0, 0, i))
    return pl.pallas_call(
        _peer_expert_kernel,
        grid=(t // tb, ne // ec),
        in_specs=[
            pl.BlockSpec((tb, d), lambda i, c: (i, 0)),
            pl.BlockSpec((ec, d), lambda i, c: (c, 0)),
            pl.BlockSpec((ec, d), lambda i, c: (c, 0)),
            sspec, sspec,
            pl.BlockSpec((PEER_HEADS, SUBLANE, tb), lambda i, c: (0, 0, i)),
            pl.BlockSpec((tb, d), lambda i, c: (i, 0)),
        ],
        out_specs=pl.BlockSpec((tb, d), lambda i, c: (i, 0)),
        out_shape=jax.ShapeDtypeStruct((t, d), F32),
        scratch_shapes=[pltpu.VMEM((PEER_HEADS, PEER_NKEYS, tb), F32)] * 3
        + [pltpu.VMEM((ec, tb), F32), pltpu.VMEM((ec, tb), BF16), pltpu.VMEM((tb, d), F32)],
        compiler_params=_cparams(("parallel", "arbitrary")),
        name="peer_expert",
    )(xn, u, v, s1, s2, st, x2d)


def _final_norm_kernel(x_ref, g_ref, o_ref):
    o_ref[...] = _rms(x_ref[...], g_ref[...])


def _final_norm(x2d, g):
    t, d = x2d.shape
    tm = _pick(t, 512)
    return pl.pallas_call(
        _final_norm_kernel,
        grid=(t // tm,),
        in_specs=[pl.BlockSpec((tm, d), lambda i: (i, 0)), pl.BlockSpec((1, d), lambda i: (0, 0))],
        out_specs=pl.BlockSpec((tm, d), lambda i: (i, 0)),
        out_shape=jax.ShapeDtypeStruct((t, d), F32),
        compiler_params=_cparams(("parallel",)),
        name="final_norm",
    )(x2d, g.reshape(1, d))


def _split_w_in(w_in, lo, hi):
    main = jnp.concatenate([w_in[:, :lo], w_in[:, hi:]], axis=1).astype(BF16)
    small = jnp.pad(w_in[:, lo:hi], ((0, 0), (0, LANE - (hi - lo)))).astype(BF16)
    return main, small


def _head_rows(conv_buf):
    b, r, w = conv_buf.shape
    return jnp.concatenate([jnp.zeros((b, SUBLANE - r, w), conv_buf.dtype), conv_buf], axis=1)


def _peer(x2d, g_ffn, wq, k1, k2, u, v):
    xn, s1, s2 = _peer_query(x2d, g_ffn, wq, k1, k2)
    st = _peer_topk(s1, s2)
    return _peer_expert(xn, u, v, s1, s2, st, x2d)


def _layer(kind, x, mem, state, mp, g_mix, peer):
    b, l, d = x.shape
    assert l >= CONV_W - 1
    x2d = x.reshape(b * l, d)
    proj2d, small2d = _norm_matmul(x2d, g_mix, mp["w_main"], mp["w_small"])
    proj = proj2d.reshape(b, l, -1)
    small = small2d.reshape(b, l, LANE)
    ng = None
    if kind == "delta":
        s0, conv_buf = state
        qkv = _conv_silu(proj, 0, _head_rows(conv_buf), mp["conv"], jnp.zeros((3 * DN_W,), F32))
        o_mix, s_new = _delta_scan(qkv, proj, small, mp["alog"], mp["dtb"], mp["ng"], s0)
        new_state = (s_new, proj[:, l - (CONV_W - 1):, :3 * DN_W])
        qblk = (4 * DN_W) // MEM_W
    elif kind == "fox":
        f_cols = small[..., :FOX_HEADS]
        z_new = f_cols.transpose(1, 0, 2).reshape(l, b * FOX_HEADS)
        bias_row = jnp.tile(mp["bf"], b).reshape(1, b * FOX_HEADS)
        if state is None:
            past = 0
            zall = z_new
        else:
            ck, cv, clf = state
            past = ck.shape[1]
            zall = jnp.concatenate([clf.transpose(1, 0, 2).reshape(past, b * FOX_HEADS), z_new], axis=0)
        logf_all, cum = _fox_gate(zall, bias_row, past)
        cum3 = cum.reshape(past + l, b, FOX_HEADS)
        cum_bl = cum3.transpose(1, 0, 2)
        cum_bt = cum3.transpose(1, 2, 0)
        if state is None:
            o_mix = _fox_attn(proj, cum_bl, cum_bt)
        else:
            o_mix = _fox_dec(proj, ck.reshape(b, past, FOX_W), cv.reshape(b, past, FOX_W),
                             cum_bl[:, past:], cum_bt[:, :, :past], cum_bt[:, :, past:])
        logf_new = logf_all[past:].reshape(l, b, FOX_HEADS).transpose(1, 0, 2)
        new_state = (proj[..., FOX_W:2 * FOX_W].reshape(b, l, FOX_HEADS, FOX_HD),
                     proj[..., 2 * FOX_W:3 * FOX_W].reshape(b, l, FOX_HEADS, FOX_HD), logf_new)
        qblk = (3 * FOX_W) // MEM_W
    else:
        h0, conv_buf = state
        xbc = _conv_silu(proj, SSD_DINNER, _head_rows(conv_buf), mp["conv"], mp["convb"])
        o_mix, h_new = _ssd_scan(xbc, proj, small, mp["alog"], mp["dtb"], mp["d"], h0)
        new_state = (h_new, proj[:, l - (CONV_W - 1):, SSD_DINNER:SSD_DINNER + SSD_CONV_DIM])
        ng = mp["ng"]
        qblk = (SSD_DINNER + SSD_CONV_DIM) // MEM_W
    o_mem = _mem_attn(proj, qblk, *mem)
    x2d = _out_proj(o_mix.reshape(b * l, -1), o_mem.reshape(b * l, MEM_W), mp["w_out"], x2d, ng)
    x2d = _peer(x2d, *peer)
    return x2d.reshape(b, l, d), new_state


_MIXERS = ("delta", "fox", "ssd")


def kernel(x_prompt, x_sample, cache_mem_k, cache_mem_v, state_l0_delta, state_l0_conv, cache_l1_k, cache_l1_v, cache_l1_logf, state_l2_ssm, state_l2_conv, state_l3_delta, state_l3_conv, mem_prompt, g_mix, g_mem, w_mem_k, w_mem_v, g_ffn, peer_wq, peer_k1, peer_k2, peer_u, peer_v, w_in_l0, w_out_l0, dn_conv_l0, dn_alog_l0, dn_dtb_l0, dn_ng_l0, w_in_l1, w_out_l1, fox_bf_l1, w_in_l2, w_out_l2, ssd_conv_l2, ssd_convb_l2, ssd_alog_l2, ssd_dtb_l2, ssd_d_l2, ssd_ng_l2, w_in_l3, w_out_l3, dn_conv_l3, dn_alog_l3, dn_dtb_l3, dn_ng_l3, g_final):
    depth = g_mix.shape[0]
    bp, _, d = x_prompt.shape
    bs = x_sample.shape[0]
    n_mem = mem_prompt.shape[1]

    def dn_params(w_in, w_out, conv, alog, dtb, ng):
        main, small = _split_w_in(w_in, 4 * DN_W, 4 * DN_W + 2 * DN_HEADS)
        return {"w_main": main, "w_small": small, "w_out": w_out.astype(BF16), "conv": conv,
                "alog": alog, "dtb": dtb, "ng": ng}

    fox_main, fox_small = _split_w_in(w_in_l1, 3 * FOX_W, 3 * FOX_W + FOX_HEADS)
    ssd_main, ssd_small = _split_w_in(w_in_l2, SSD_DINNER + SSD_CONV_DIM, SSD_DINNER + SSD_CONV_DIM + SSD_HEADS)
    mix_params = (
        dn_params(w_in_l0, w_out_l0, dn_conv_l0, dn_alog_l0, dn_dtb_l0, dn_ng_l0),
        {"w_main": fox_main, "w_small": fox_small, "w_out": w_out_l1.astype(BF16), "bf": fox_bf_l1},
        {"w_main": ssd_main, "w_small": ssd_small, "w_out": w_out_l2.astype(BF16), "conv": ssd_conv_l2,
         "convb": ssd_convb_l2, "alog": ssd_alog_l2, "dtb": ssd_dtb_l2, "d": ssd_d_l2, "ng": ssd_ng_l2},
        dn_params(w_in_l3, w_out_l3, dn_conv_l3, dn_alog_l3, dn_dtb_l3, dn_ng_l3),
    )
    peer_params = [(g_ffn[i], peer_wq[i].astype(BF16), peer_k1[i].astype(BF16), peer_k2[i].astype(BF16),
                    peer_u[i].astype(BF16), peer_v[i].astype(BF16)) for i in range(depth)]

    def zeros_states(b):
        return (
            (jnp.zeros((b, DN_HEADS, DN_HD, DN_HD), F32), jnp.zeros((b, CONV_W - 1, 3 * DN_W), F32)),
            None,
            (jnp.zeros((b, SSD_HEADS, SSD_HD, SSD_N), F32), jnp.zeros((b, CONV_W - 1, SSD_CONV_DIM), F32)),
            (jnp.zeros((b, DN_HEADS, DN_HD, DN_HD), F32), jnp.zeros((b, CONV_W - 1, 3 * DN_W), F32)),
        )

    prompt_states = zeros_states(bp)
    sample_states = (
        (state_l0_delta, state_l0_conv),
        (cache_l1_k, cache_l1_v, cache_l1_logf),
        (state_l2_ssm, state_l2_conv),
        (state_l3_delta, state_l3_conv),
    )

    mem2d = mem_prompt.reshape(bp * n_mem, d)
    zero_small = jnp.zeros((d, LANE), BF16)

    x = x_prompt
    p_mk, p_mv, p_new = [], [], []
    for i in range(depth):
        w_kv = jnp.concatenate([w_mem_k[i], w_mem_v[i]], axis=1).astype(BF16)
        kv2d, _ = _norm_matmul(mem2d, g_mem[i], w_kv, zero_small)
        kv = kv2d.reshape(bp, n_mem, 2 * MEM_W)
        p_mk.append(kv[..., :MEM_W].reshape(bp, n_mem, MEM_HEADS, MEM_HD))
        p_mv.append(kv[..., MEM_W:].reshape(bp, n_mem, MEM_HEADS, MEM_HD))
        x, st = _layer(_MIXERS[i % 3], x, (kv, 0, kv, 1), prompt_states[i], mix_params[i], g_mix[i], peer_params[i])
        p_new.append(st)
    y_prompt = _final_norm(x.reshape(-1, d), g_final).reshape(x.shape)

    xs = x_sample
    s_new = []
    for i in range(depth):
        mk = cache_mem_k[i].reshape(bs, n_mem, MEM_W)
        mv = cache_mem_v[i].reshape(bs, n_mem, MEM_W)
        xs, st = _layer(_MIXERS[i % 3], xs, (mk, 0, mv, 0), sample_states[i], mix_params[i], g_mix[i], peer_params[i])
        s_new.append(st)
    y_sample = _final_norm(xs.reshape(-1, d), g_final).reshape(xs.shape)

    p_mem_k = jnp.stack(p_mk)
    p_mem_v = jnp.stack(p_mv)
    p_l0_delta, p_l0_conv = p_new[0]
    p_l1_k, p_l1_v, p_l1_logf = p_new[1]
    p_l2_ssm, p_l2_conv = p_new[2]
    p_l3_delta, p_l3_conv = p_new[3]
    s_l0_delta, s_l0_conv = s_new[0]
    s_l1_k, s_l1_v, s_l1_logf = s_new[1]
    s_l2_ssm, s_l2_conv = s_new[2]
    s_l3_delta, s_l3_conv = s_new[3]
    return (y_prompt, y_sample, p_mem_k, p_mem_v,
            p_l0_delta, p_l0_conv, p_l1_k, p_l1_v, p_l1_logf, p_l2_ssm, p_l2_conv, p_l3_delta, p_l3_conv,
            s_l0_delta, s_l0_conv, s_l1_k, s_l1_v, s_l1_logf, s_l2_ssm, s_l2_conv, s_l3_delta, s_l3_conv)
```
